```python
import jax, jax.numpy as jnp
from jax import lax
import numpy as np

D_MODEL = 4096
BATCH = 4
SEQ = 2048
DEPTH = 2
DEC_BATCH = 128
DEC_SEQ = 8
PAST_LEN = 16384
PAGE_SIZE = 128

N_EVEN = (DEPTH + 1) // 2
N_ODD = DEPTH // 2
A_W = D_MODEL
A_HEADS = 8
A_HD = A_W // A_HEADS
CHUNK = 128
B_W = D_MODEL
B_HEADS = 8
B_HD = B_W // B_HEADS
CONV_W = 31
C_W = 2 * D_MODEL
POOL_WINDOWS = (2, 4, 8, 16)
POOL_GROUPS = len(POOL_WINDOWS)
C_G = C_W // POOL_GROUPS
POOL_MAX = max(POOL_WINDOWS)

EVEN_IN = 3 * A_W + 3 * B_W
ODD_IN = 2 * C_W
RMS_EPS = 1e-6
LN_EPS = 1e-5

kernel_name = "hybrid_sgu_conv_pool_decoder_step"


def _rmsnorm(x, g):
    xf = x.astype(jnp.float32)
    y = xf * lax.rsqrt(jnp.mean(xf * xf, axis=-1, keepdims=True) + RMS_EPS)
    return (y * g.astype(jnp.float32)).astype(x.dtype)


def _layernorm(x, g, b):
    xf = x.astype(jnp.float32)
    mu = jnp.mean(xf, axis=-1, keepdims=True)
    var = jnp.mean(jnp.square(xf - mu), axis=-1, keepdims=True)
    y = (xf - mu) * lax.rsqrt(var + LN_EPS)
    return (y * g.astype(jnp.float32) + b.astype(jnp.float32)).astype(x.dtype)


def _spatial_gate(u, v, w_s, b_s):
    n, L, _ = v.shape
    cl = min(L, CHUNK)
    nc = L // cl
    mask = jnp.tril(jnp.ones((cl, cl), dtype=bool))
    w = jnp.where(mask[None], w_s[:, :cl, :cl], jnp.zeros((), w_s.dtype)).astype(v.dtype)
    vc = v.reshape(n, nc, cl, A_HEADS, A_HD)
    mixed = jnp.einsum('hts,ncshd->ncthd', w, vc)
    mixed = mixed + b_s[:, :cl].T.astype(v.dtype)[None, None, :, :, None]
    return u * mixed.reshape(n, L, A_W)


def _causal_dwconv(ext, w, b):
    y = lax.conv_general_dilated(
        ext, w[:, None, :].astype(ext.dtype), window_strides=(1,), padding='VALID',
        dimension_numbers=('NWC', 'WIO', 'NWC'), feature_group_count=ext.shape[-1])
    return y + b.astype(ext.dtype)


def _even_layer(x, conv_prefix, g_in, w_in, sgu_w, sgu_b, v_g, v_b, conv_w, conv_b, cn_g, cn_b, w_out):
    n, L, _ = x.shape
    h = _rmsnorm(x, g_in)
    z = h @ w_in.astype(x.dtype)
    u, v, gate_a, glu_a, glu_b, gate_b = jnp.split(
        z, [A_W, 2 * A_W, 3 * A_W, 3 * A_W + B_W, 3 * A_W + 2 * B_W], axis=-1)
    u = jax.nn.gelu(u, approximate=False)
    v = _layernorm(jax.nn.gelu(v, approximate=False), v_g, v_b)
    y_a = _spatial_gate(u, v, sgu_w, sgu_b) * jax.nn.silu(gate_a)
    glu = glu_a * jax.nn.sigmoid(glu_b)
    ext = jnp.concatenate([conv_prefix.astype(glu.dtype), glu], axis=1)
    c = _causal_dwconv(ext, conv_w, conv_b)
    c = _layernorm(c.reshape(n, L, B_HEADS, B_HD), cn_g.reshape(B_HEADS, B_HD),
                   cn_b.reshape(B_HEADS, B_HD)).reshape(n, L, B_W)
    y_b = jax.nn.silu(c) * jax.nn.silu(gate_b)
    y = jnp.concatenate([y_a, y_b], axis=-1) @ w_out.astype(x.dtype)
    return x + y, v, ext[:, -(CONV_W - 1):]


def _odd_layer(x, pool_prefix, pos0, g_in, w_in, pool_w, pool_scale, w_out):
    n, L, _ = x.shape
    h = _rmsnorm(x, g_in)
    z = h @ w_in.astype(x.dtype)
    xc, gate_c = jnp.split(z, [C_W], axis=-1)
    ext = jnp.concatenate([pool_prefix.astype(xc.dtype), xc], axis=1)
    cs = jnp.cumsum(ext.astype(jnp.float32), axis=1)
    cs = jnp.concatenate([jnp.zeros((n, 1, C_W), jnp.float32), cs], axis=1)
    pos = pos0 + jnp.arange(L)
    diffs = []
    for gi, w in enumerate(POOL_WINDOWS):
        sl = slice(gi * C_G, (gi + 1) * C_G)
        s = cs[:, POOL_MAX:POOL_MAX + L, sl] - cs[:, POOL_MAX - w:POOL_MAX - w + L, sl]
        cnt = jnp.minimum(w, pos + 1).astype(jnp.float32)[None, :, None]
        diffs.append((s / cnt).astype(xc.dtype) - xc[..., sl])
    d = jnp.stack(diffs, axis=2)
    y_c = jnp.einsum('nlgc,gcd->nlgd', d, pool_w.astype(xc.dtype)).reshape(n, L, C_W)
    y_c = y_c * pool_scale.astype(xc.dtype) * jax.nn.silu(gate_c)
    y = y_c @ w_out.astype(x.dtype)
    return x + y, ext[:, -(POOL_MAX - 1):]


def setup_inputs(seed: int = 0) -> dict:
    key = jax.random.key(seed)
    ks = jax.random.split(key, 24)
    f32 = jnp.float32
    nrm = lambda k, shape, s: jax.random.normal(k, shape, f32) * s
    return {
        "x_prompt": nrm(ks[0], (BATCH, SEQ, D_MODEL), 1.0),
        "x_sample": nrm(ks[1], (DEC_BATCH, DEC_SEQ, D_MODEL), 1.0),
        "state_conv": nrm(ks[2], (N_EVEN, DEC_BATCH, CONV_W - 1, B_W), 0.5),
        "state_pool": nrm(ks[3], (N_ODD, DEC_BATCH, POOL_MAX - 1, C_W), 1.0),
        "norm_in": 1.0 + nrm(ks[4], (DEPTH, D_MODEL), 0.02),
        "w_in_even": nrm(ks[5], (N_EVEN, D_MODEL, EVEN_IN), D_MODEL ** -0.5),
        "sgu_w": nrm(ks[6], (N_EVEN, A_HEADS, CHUNK, CHUNK), 0.5 * CHUNK ** -0.5),
        "sgu_b": 1.0 + nrm(ks[7], (N_EVEN, A_HEADS, CHUNK), 0.01),
        "v_norm_g": 1.0 + nrm(ks[8], (N_EVEN, A_W), 0.02),
        "v_norm_b": nrm(ks[9], (N_EVEN, A_W), 0.02),
        "conv_w": nrm(ks[10], (N_EVEN, CONV_W, B_W), CONV_W ** -0.5),
        "conv_b": nrm(ks[11], (N_EVEN, B_W), 0.02),
        "conv_norm_g": 1.0 + nrm(ks[12], (N_EVEN, B_W), 0.02),
        "conv_norm_b": nrm(ks[13], (N_EVEN, B_W), 0.02),
        "w_out_even": nrm(ks[14], (N_EVEN, A_W + B_W, D_MODEL), (A_W + B_W) ** -0.5),
        "w_in_odd": nrm(ks[15], (N_ODD, D_MODEL, ODD_IN), D_MODEL ** -0.5),
        "pool_w": nrm(ks[16], (N_ODD, POOL_GROUPS, C_G, C_G), C_G ** -0.5),
        "pool_scale": 1.0 + nrm(ks[17], (N_ODD, C_W), 0.02),
        "w_out_odd": nrm(ks[18], (N_ODD, C_W, D_MODEL), C_W ** -0.5),
        "norm_f": 1.0 + nrm(ks[19], (D_MODEL,), 0.02),
    }


def reference(x_prompt, x_sample, state_conv, state_pool, norm_in, w_in_even, sgu_w, sgu_b,
              v_norm_g, v_norm_b, conv_w, conv_b, conv_norm_g, conv_norm_b, w_out_even,
              w_in_odd, pool_w, pool_scale, w_out_odd, norm_f):
    xp, xs = x_prompt, x_sample
    n_p = x_prompt.shape[0]
    chunk_v_s, conv_p, conv_s, pool_p, pool_s = [], [], [], [], []
    for layer in range(DEPTH):
        if layer % 2 == 0:
            i = layer // 2
            prm = (norm_in[layer], w_in_even[i], sgu_w[i], sgu_b[i], v_norm_g[i], v_norm_b[i],
                   conv_w[i], conv_b[i], conv_norm_g[i], conv_norm_b[i], w_out_even[i])
            zero_conv = jnp.zeros((n_p, CONV_W - 1, B_W), xp.dtype)
            xp, _, cp = _even_layer(xp, zero_conv, *prm)
            xs, vs, cs = _even_layer(xs, state_conv[i], *prm)
            chunk_v_s.append(vs)
            conv_p.append(cp)
            conv_s.append(cs)
        else:
            i = layer // 2
            prm = (norm_in[layer], w_in_odd[i], pool_w[i], pool_scale[i], w_out_odd[i])
            zero_pool = jnp.zeros((n_p, POOL_MAX - 1, C_W), xp.dtype)
            xp, pp = _odd_layer(xp, zero_pool, 0, *prm)
            xs, ps = _odd_layer(xs, state_pool[i], PAST_LEN, *prm)
            pool_p.append(pp)
            pool_s.append(ps)
    y_prompt = _rmsnorm(xp, norm_f)
    y_sample = _rmsnorm(xs, norm_f)
    return (y_prompt, y_sample, jnp.stack(chunk_v_s), jnp.stack(conv_p), jnp.stack(conv_s),
            jnp.stack(pool_p), jnp.stack(pool_s))
```

```python
import functools

import jax
import jax.numpy as jnp
from jax import lax
from jax.experimental import pallas as pl
from jax.experimental.pallas import tpu as pltpu

F32 = jnp.float32
BF16 = jnp.bfloat16

D_MODEL = 4096
HEADS = 8
HEAD_DIM = D_MODEL // HEADS
CHUNK = 128
CONV_W = 31
CONV_HALO = CONV_W - 1
CONV_CARRY = 32
POOL_WINDOWS = (2, 4, 8, 16)
POOL_HALO = max(POOL_WINDOWS) - 1
POOL_CARRY = 16
C_W = 2 * D_MODEL
POOL_BLOCKS_PER_GROUP = (C_W // len(POOL_WINDOWS)) // HEAD_DIM
RMS_EPS = 1e-6
LN_EPS = 1e-5

TM = 512
SEQ_TILE = TM // 8
ROW_CHUNK = 64
VMEM_LIMIT = 56 * 1024 * 1024


def _params():
    return pltpu.CompilerParams(
        dimension_semantics=("arbitrary", "arbitrary"), vmem_limit_bytes=VMEM_LIMIT)


def _sigmoid(x):
    return 1.0 / (1.0 + jnp.exp(-x))


def _silu(x):
    return x * _sigmoid(x)


def _gelu(x):
    return 0.5 * x * (1.0 + lax.erf(x * 0.7071067811865476))


def _dot(a, b):
    return jnp.dot(a, b, preferred_element_type=F32)


def _rmsnorm_kernel(x_ref, g_ref, o_ref):
    x = x_ref[...]
    ms = jnp.mean(x * x, axis=-1, keepdims=True)
    o_ref[...] = (x * lax.rsqrt(ms + RMS_EPS) * g_ref[...]).astype(o_ref.dtype)


def _rmsnorm(x, g, out_dtype):
    rows = x.shape[0]
    tr = 256
    return pl.pallas_call(
        _rmsnorm_kernel,
        grid=(rows // tr,),
        in_specs=[pl.BlockSpec((tr, D_MODEL), lambda m: (m, 0)),
                  pl.BlockSpec((1, D_MODEL), lambda m: (0, 0))],
        out_specs=pl.BlockSpec((tr, D_MODEL), lambda m: (m, 0)),
        out_shape=jax.ShapeDtypeStruct((rows, D_MODEL), out_dtype),
        compiler_params=pltpu.CompilerParams(
            dimension_semantics=("arbitrary",), vmem_limit_bytes=VMEM_LIMIT),
        name="rmsnorm",
    )(x, g.reshape(1, D_MODEL))


def _v_kernel(h_ref, w_ref, g_ref, b_ref, o_ref):
    n = pl.program_id(1)
    o_ref[n] = _gelu(_dot(h_ref[...], w_ref[...]))

    @pl.when(n == HEADS - 1)
    def _():
        inv_w = 1.0 / D_MODEL

        def body(r, carry):
            rows = pl.ds(pl.multiple_of(r * ROW_CHUNK, ROW_CHUNK), ROW_CHUNK)
            v = o_ref[:, rows, :]
            mu = jnp.sum(jnp.sum(v, axis=0), axis=-1, keepdims=True) * inv_w
            c = v - mu[None]
            var = jnp.sum(jnp.sum(c * c, axis=0), axis=-1, keepdims=True) * inv_w
            o_ref[:, rows, :] = c * lax.rsqrt(var + LN_EPS)[None] * g_ref[...] + b_ref[...]
            return carry

        lax.fori_loop(0, TM // ROW_CHUNK, body, 0)


def _v_path(h, w_in, v_g, v_b):
    rows = h.shape[0]
    return pl.pallas_call(
        _v_kernel,
        grid=(rows // TM, HEADS),
        in_specs=[pl.BlockSpec((TM, D_MODEL), lambda m, n: (m, 0)),
                  pl.BlockSpec((D_MODEL, HEAD_DIM), lambda m, n: (0, HEADS + n)),
                  pl.BlockSpec((HEADS, 1, HEAD_DIM), lambda m, n: (0, 0, 0)),
                  pl.BlockSpec((HEADS, 1, HEAD_DIM), lambda m, n: (0, 0, 0))],
        out_specs=pl.BlockSpec((HEADS, TM, HEAD_DIM), lambda m, n: (0, m, 0)),
        out_shape=jax.ShapeDtypeStruct((HEADS, rows, HEAD_DIM), F32),
        compiler_params=_params(),
        name="v_path",
    )(h, w_in, v_g.reshape(HEADS, 1, HEAD_DIM), v_b.reshape(HEADS, 1, HEAD_DIM))


def _sgu_kernel(h_ref, wu_ref, wg_ref, v_ref, mix_ref, bias_ref, o_ref, *, chunk):
    h = h_ref[...]
    u = _dot(h, wu_ref[...])
    ga = _dot(h, wg_ref[...])
    for c in range(TM // chunk):
        rows = slice(c * chunk, (c + 1) * chunk)
        mixed = _dot(mix_ref[...], v_ref[rows, :].astype(BF16)) + bias_ref[...]
        o_ref[rows, :] = (_gelu(u[rows]) * mixed * _silu(ga[rows])).astype(o_ref.dtype)


def _sgu_branch(h, w_in, v, mix, bias):
    rows = h.shape[0]
    chunk = mix.shape[-1]
    return pl.pallas_call(
        functools.partial(_sgu_kernel, chunk=chunk),
        grid=(rows // TM, HEADS),
        in_specs=[pl.BlockSpec((TM, D_MODEL), lambda m, j: (m, 0)),
                  pl.BlockSpec((D_MODEL, HEAD_DIM), lambda m, j: (0, j)),
                  pl.BlockSpec((D_MODEL, HEAD_DIM), lambda m, j: (0, 2 * HEADS + j)),
                  pl.BlockSpec((None, TM, HEAD_DIM), lambda m, j: (j, m, 0)),
                  pl.BlockSpec((None, chunk, chunk), lambda m, j: (j, 0, 0)),
                  pl.BlockSpec((None, chunk, HEAD_DIM), lambda m, j: (j, 0, 0))],
        out_specs=pl.BlockSpec((TM, HEAD_DIM), lambda m, j: (m, j)),
        out_shape=jax.ShapeDtypeStruct((rows, D_MODEL), BF16),
        compiler_params=_params(),
        name="sgu_branch",
    )(h, w_in, w_in, v, mix, bias)


def _group_norm_swish_gate(c, gb, ng, nb):
    mu = jnp.mean(c, axis=-1, keepdims=True)
    d = c - mu
    var = jnp.mean(d * d, axis=-1, keepdims=True)
    cn = d * lax.rsqrt(var + LN_EPS) * ng + nb
    return _silu(cn) * _silu(gb)


def _conv_inputs(h_ref, wa_ref, wb_ref, wg_ref):
    h = h_ref[...]
    glu = _dot(h, wa_ref[...]) * _sigmoid(_dot(h, wb_ref[...]))
    gb = _dot(h, wg_ref[...])
    return glu, gb


def _conv_kernel_prompt(h_ref, wa_ref, wb_ref, wg_ref, cw_ref, cb_ref, ng_ref, nb_ref,
                        y_ref, st_ref, ext, carry_ref, gb_ref, *, tiles_per_seq):
    m = pl.program_id(0)
    j = pl.program_id(1)
    glu, gb = _conv_inputs(h_ref, wa_ref, wb_ref, wg_ref)
    first = m % tiles_per_seq == 0

    @pl.when(first)
    def _():
        ext[0:CONV_CARRY, :] = jnp.zeros((CONV_CARRY, HEAD_DIM), F32)

    @pl.when(jnp.logical_not(first))
    def _():
        ext[0:CONV_CARRY, :] = carry_ref[j]

    ext[CONV_CARRY:CONV_CARRY + TM, :] = glu
    gb_ref[...] = gb
    pad = CONV_CARRY - CONV_HALO
    for r in range(TM // ROW_CHUNK):
        r0 = r * ROW_CHUNK
        acc = cw_ref[0:1, :] * ext[r0 + pad:r0 + pad + ROW_CHUNK, :]
        for k in range(1, CONV_W):
            acc = acc + cw_ref[k:k + 1, :] * ext[r0 + pad + k:r0 + pad + k + ROW_CHUNK, :]
        c = acc + cb_ref[...]
        y = _group_norm_swish_gate(c, gb_ref[r0:r0 + ROW_CHUNK, :], ng_ref[...], nb_ref[...])
        y_ref[r0:r0 + ROW_CHUNK, :] = y.astype(y_ref.dtype)
    st_ref[...] = ext[CONV_CARRY + TM - CONV_HALO:CONV_CARRY + TM, :]
    carry_ref[j] = ext[TM:TM + CONV_CARRY, :]


def _conv_branch_prompt(h, w_in, conv_w, conv_b, cn_g, cn_b, seq_len):
    rows = h.shape[0]
    tiles_per_seq = seq_len // TM
    seg = D_MODEL // HEAD_DIM
    vec = lambda a: a.reshape(1, D_MODEL)
    vspec = pl.BlockSpec((1, HEAD_DIM), lambda m, j: (0, j))
    y_b, tails = pl.pallas_call(
        functools.partial(_conv_kernel_prompt, tiles_per_seq=tiles_per_seq),
        grid=(rows // TM, HEADS),
        in_specs=[pl.BlockSpec((TM, D_MODEL), lambda m, j: (m, 0)),
                  pl.BlockSpec((D_MODEL, HEAD_DIM), lambda m, j: (0, 3 * seg + j)),
                  pl.BlockSpec((D_MODEL, HEAD_DIM), lambda m, j: (0, 4 * seg + j)),
                  pl.BlockSpec((D_MODEL, HEAD_DIM), lambda m, j: (0, 5 * seg + j)),
                  pl.BlockSpec((CONV_W, HEAD_DIM), lambda m, j: (0, j)),
                  vspec, vspec, vspec],
        out_specs=[pl.BlockSpec((TM, HEAD_DIM), lambda m, j: (m, j)),
                   pl.BlockSpec((None, CONV_HALO, HEAD_DIM), lambda m, j: (m, 0, j))],
        out_shape=[jax.ShapeDtypeStruct((rows, D_MODEL), BF16),
                   jax.ShapeDtypeStruct((rows // TM, CONV_HALO, D_MODEL), F32)],
        scratch_shapes=[pltpu.VMEM((CONV_CARRY + TM, HEAD_DIM), F32),
                        pltpu.VMEM((HEADS, CONV_CARRY, HEAD_DIM), F32),
                        pltpu.VMEM((TM, HEAD_DIM), F32)],
        compiler_params=_params(),
        name="conv_branch_prompt",
    )(h, w_in, w_in, w_in, conv_w, vec(conv_b), vec(cn_g), vec(cn_b))
    return y_b, tails[tiles_per_seq - 1::tiles_per_seq]


def _conv_kernel_sample(h_ref, wa_ref, wb_ref, wg_ref, cw_ref, cb_ref, ng_ref, nb_ref,
                        state_ref, y_ref, st_ref, glu_ref, *, steps):
    glu, gb = _conv_inputs(h_ref, wa_ref, wb_ref, wg_ref)
    glu_ref[...] = glu

    def ext_row(i):
        if i < CONV_HALO:
            return state_ref[:, i, :]
        return glu_ref[(i - CONV_HALO) * SEQ_TILE:(i - CONV_HALO + 1) * SEQ_TILE, :]

    for t in range(steps):
        acc = cw_ref[0:1, :] * ext_row(t)
        for k in range(1, CONV_W):
            acc = acc + cw_ref[k:k + 1, :] * ext_row(t + k)
        c = acc + cb_ref[...]
        rows = slice(t * SEQ_TILE, (t + 1) * SEQ_TILE)
        y = _group_norm_swish_gate(c, gb[rows], ng_ref[...], nb_ref[...])
        y_ref[rows, :] = y.astype(y_ref.dtype)
    keep = CONV_HALO - steps
    st_ref[:, 0:keep, :] = state_ref[:, steps:CONV_HALO, :]
    for t in range(steps):
        st_ref[:, keep + t, :] = glu_ref[t * SEQ_TILE:(t + 1) * SEQ_TILE, :]


def _conv_branch_sample(h, w_in, conv_w, conv_b, cn_g, cn_b, state, steps):
    rows = h.shape[0]
    n_seq = state.shape[0]
    seg = D_MODEL // HEAD_DIM
    vec = lambda a: a.reshape(1, D_MODEL)
    vspec = pl.BlockSpec((1, HEAD_DIM), lambda m, j: (0, j))
    st_spec = pl.BlockSpec((SEQ_TILE, CONV_HALO, HEAD_DIM), lambda m, j: (m, 0, j))
    return pl.pallas_call(
        functools.partial(_conv_kernel_sample, steps=steps),
        grid=(rows // TM, HEADS),
        in_specs=[pl.BlockSpec((TM, D_MODEL), lambda m, j: (m, 0)),
                  pl.BlockSpec((D_MODEL, HEAD_DIM), lambda m, j: (0, 3 * seg + j)),
                  pl.BlockSpec((D_MODEL, HEAD_DIM), lambda m, j: (0, 4 * seg + j)),
                  pl.BlockSpec((D_MODEL, HEAD_DIM), lambda m, j: (0, 5 * seg + j)),
                  pl.BlockSpec((CONV_W, HEAD_DIM), lambda m, j: (0, j)),
                  vspec, vspec, vspec, st_spec],
        out_specs=[pl.BlockSpec((TM, HEAD_DIM), lambda m, j: (m, j)), st_spec],
        out_shape=[jax.ShapeDtypeStruct((rows, D_MODEL), BF16),
                   jax.ShapeDtypeStruct((n_seq, CONV_HALO, D_MODEL), F32)],
        scratch_shapes=[pltpu.VMEM((TM, HEAD_DIM), F32)],
        compiler_params=_params(),
        name="conv_branch_sample",
    )(h, w_in, w_in, w_in, conv_w, vec(conv_b), vec(cn_g), vec(cn_b), state)


def _out_proj_kernel(ya_ref, yb_ref, wa_ref, wb_ref, x_ref, o_ref):
    o_ref[...] = x_ref[...] + (_dot(ya_ref[...], wa_ref[...]) + _dot(yb_ref[...], wb_ref[...]))


def _out_proj(y_left, y_right, right_block, w, x):
    rows = x.shape[0]
    return pl.pallas_call(
        _out_proj_kernel,
        grid=(rows // TM, D_MODEL // HEAD_DIM),
        in_specs=[pl.BlockSpec((TM, D_MODEL), lambda m, n: (m, 0)),
                  pl.BlockSpec((TM, D_MODEL), lambda m, n: (m, right_block)),
                  pl.BlockSpec((D_MODEL, HEAD_DIM), lambda m, n: (0, n)),
                  pl.BlockSpec((D_MODEL, HEAD_DIM), lambda m, n: (1, n)),
                  pl.BlockSpec((TM, HEAD_DIM), lambda m, n: (m, n))],
        out_specs=pl.BlockSpec((TM, HEAD_DIM), lambda m, n: (m, n)),
        out_shape=jax.ShapeDtypeStruct((rows, D_MODEL), F32),
        compiler_params=_params(),
        name="out_proj",
    )(y_left, y_right, w, w, x)


def _pool_diff(window_sum, xc, window, first_pos):
    pos = first_pos + lax.broadcasted_iota(jnp.int32, xc.shape, 0)
    cnt = jnp.minimum(window, pos + 1).astype(F32)
    return window_sum / cnt - xc


def _pool_kernel_prompt(h_ref, w_ref, d_ref, st_ref, ext, carry_ref, *, tiles_per_seq):
    m = pl.program_id(0)
    j = pl.program_id(1)
    xc = _dot(h_ref[...], w_ref[...])
    first = m % tiles_per_seq == 0

    @pl.when(first)
    def _():
        ext[0:POOL_CARRY, :] = jnp.zeros((POOL_CARRY, HEAD_DIM), F32)

    @pl.when(jnp.logical_not(first))
    def _():
        ext[0:POOL_CARRY, :] = carry_ref[j]

    ext[POOL_CARRY:POOL_CARRY + TM, :] = xc
    first_pos = (m % tiles_per_seq) * TM
    for gi, window in enumerate(POOL_WINDOWS):
        @pl.when(j // POOL_BLOCKS_PER_GROUP == gi)
        def _():
            for r in range(TM // ROW_CHUNK):
                r0 = POOL_CARRY + r * ROW_CHUNK
                cur = ext[r0:r0 + ROW_CHUNK, :]
                s = cur
                for i in range(1, window):
                    s = s + ext[r0 - i:r0 - i + ROW_CHUNK, :]
                d = _pool_diff(s, cur, window, first_pos + r * ROW_CHUNK)
                d_ref[r * ROW_CHUNK:(r + 1) * ROW_CHUNK, :] = d.astype(d_ref.dtype)

    st_ref[...] = ext[POOL_CARRY + TM - POOL_HALO:POOL_CARRY + TM, :]
    carry_ref[j] = ext[TM:TM + POOL_CARRY, :]


def _pool_prompt(h, w_in, seq_len):
    rows = h.shape[0]
    tiles_per_seq = seq_len // TM
    d, tails = pl.pallas_call(
        functools.partial(_pool_kernel_prompt, tiles_per_seq=tiles_per_seq),
        grid=(rows // TM, C_W // HEAD_DIM),
        in_specs=[pl.BlockSpec((TM, D_MODEL), lambda m, j: (m, 0)),
                  pl.BlockSpec((D_MODEL, HEAD_DIM), lambda m, j: (0, j))],
        out_specs=[pl.BlockSpec((TM, HEAD_DIM), lambda m, j: (m, j)),
                   pl.BlockSpec((None, POOL_HALO, HEAD_DIM), lambda m, j: (m, 0, j))],
        out_shape=[jax.ShapeDtypeStruct((rows, C_W), BF16),
                   jax.ShapeDtypeStruct((rows // TM, POOL_HALO, C_W), F32)],
        scratch_shapes=[pltpu.VMEM((POOL_CARRY + TM, HEAD_DIM), F32),
                        pltpu.VMEM((C_W // HEAD_DIM, POOL_CARRY, HEAD_DIM), F32)],
        compiler_params=_params(),
        name="pool_prompt",
    )(h, w_in)
    return d, tails[tiles_per_seq - 1::tiles_per_seq]


def _pool_kernel_sample(h_ref, w_ref, state_ref, d_ref, st_ref, xc_ref, *, steps, first_pos):
    j = pl.program_id(1)
    xc_ref[...] = _dot(h_ref[...], w_ref[...])

    def ext_row(i):
        if i < POOL_HALO:
            return state_ref[:, i, :]
        return xc_ref[(i - POOL_HALO) * SEQ_TILE:(i - POOL_HALO + 1) * SEQ_TILE, :]

    for gi, window in enumerate(POOL_WINDOWS):
        @pl.when(j // POOL_BLOCKS_PER_GROUP == gi)
        def _():
            for t in range(steps):
                cur = ext_row(POOL_HALO + t)
                s = cur
                for i in range(1, window):
                    s = s + ext_row(POOL_HALO + t - i)
                cnt = float(min(window, first_pos + t + 1))
                d_ref[t * SEQ_TILE:(t + 1) * SEQ_TILE, :] = (s / cnt - cur).astype(d_ref.dtype)

    keep = POOL_HALO - steps
    st_ref[:, 0:keep, :] = state_ref[:, steps:POOL_HALO, :]
    for t in range(steps):
        st_ref[:, keep + t, :] = xc_ref[t * SEQ_TILE:(t + 1) * SEQ_TILE, :]


def _pool_sample(h, w_in, state, steps, first_pos):
    rows = h.shape[0]
    n_seq = state.shape[0]
    st_spec = pl.BlockSpec((SEQ_TILE, POOL_HALO, HEAD_DIM), lambda m, j: (m, 0, j))
    return pl.pallas_call(
        functools.partial(_pool_kernel_sample, steps=steps, first_pos=first_pos),
        grid=(rows // TM, C_W // HEAD_DIM),
        in_specs=[pl.BlockSpec((TM, D_MODEL), lambda m, j: (m, 0)),
                  pl.BlockSpec((D_MODEL, HEAD_DIM), lambda m, j: (0, j)),
                  st_spec],
        out_specs=[pl.BlockSpec((TM, HEAD_DIM), lambda m, j: (m, j)), st_spec],
        out_shape=[jax.ShapeDtypeStruct((rows, C_W), BF16),
                   jax.ShapeDtypeStruct((n_seq, POOL_HALO, C_W), F32)],
        scratch_shapes=[pltpu.VMEM((TM, HEAD_DIM), F32)],
        compiler_params=_params(),
        name="pool_sample",
    )(h, w_in, state)


def _pool_mix_kernel(d_ref, pw_ref, h_ref, wg_ref, sc_ref, o_ref):
    yc = _dot(d_ref[...], pw_ref[...])
    gate = _dot(h_ref[...], wg_ref[...])
    o_ref[...] = (yc * sc_ref[...] * _silu(gate)).astype(o_ref.dtype)


def _pool_mix(d, pool_w, h, w_in, scale):
    rows = h.shape[0]
    group_w = C_W // len(POOL_WINDOWS)
    bpg = POOL_BLOCKS_PER_GROUP
    return pl.pallas_call(
        _pool_mix_kernel,
        grid=(rows // TM, C_W // HEAD_DIM),
        in_specs=[pl.BlockSpec((TM, group_w), lambda m, n: (m, n // bpg)),
                  pl.BlockSpec((None, group_w, HEAD_DIM), lambda m, n: (n // bpg, 0, n % bpg)),
                  pl.BlockSpec((TM, D_MODEL), lambda m, n: (m, 0)),
                  pl.BlockSpec((D_MODEL, HEAD_DIM), lambda m, n: (0, C_W // HEAD_DIM + n)),
                  pl.BlockSpec((1, HEAD_DIM), lambda m, n: (0, n))],
        out_specs=pl.BlockSpec((TM, HEAD_DIM), lambda m, n: (m, n)),
        out_shape=jax.ShapeDtypeStruct((rows, C_W), BF16),
        compiler_params=_params(),
        name="pool_mix",
    )(d, pool_w, h, w_in, scale.reshape(1, C_W))


def _even_layer(x, group, prm):
    (g_in, w_in, mix, bias, v_g, v_b, conv_w, conv_b, cn_g, cn_b, w_out) = prm
    h = _rmsnorm(x, g_in, BF16)
    v = _v_path(h, w_in, v_g, v_b)
    y_a = _sgu_branch(h, w_in, v, mix, bias)
    if group["state_conv"] is None:
        y_b, new_conv = _conv_branch_prompt(h, w_in, conv_w, conv_b, cn_g, cn_b,
                                            group["seq_len"])
    else:
        y_b, new_conv = _conv_branch_sample(h, w_in, conv_w, conv_b, cn_g, cn_b,
                                            group["state_conv"], group["seq_len"])
    return _out_proj(y_a, y_b, 0, w_out, x), v, new_conv


def _odd_layer(x, group, prm):
    (g_in, w_in, pool_w, pool_scale, w_out) = prm
    h = _rmsnorm(x, g_in, BF16)
    if group["state_pool"] is None:
        d, new_pool = _pool_prompt(h, w_in, group["seq_len"])
    else:
        d, new_pool = _pool_sample(h, w_in, group["state_pool"], group["seq_len"],
                                   group["first_pos"])
    y_c = _pool_mix(d, pool_w, h, w_in, pool_scale)
    return _out_proj(y_c, y_c, 1, w_out, x), new_pool


def _sample_rows(a):
    n_seq, steps, f = a.shape
    return a.reshape(n_seq // SEQ_TILE, SEQ_TILE, steps, f).transpose(0, 2, 1, 3).reshape(-1, f)


def _sample_unrows(a, n_seq, steps):
    f = a.shape[-1]
    return a.reshape(n_seq // SEQ_TILE, steps, SEQ_TILE, f).transpose(0, 2, 1, 3).reshape(n_seq, steps, f)


def kernel(x_prompt, x_sample, state_conv, state_pool, norm_in, w_in_even, sgu_w, sgu_b,
           v_norm_g, v_norm_b, conv_w, conv_b, conv_norm_g, conv_norm_b, w_out_even,
           w_in_odd, pool_w, pool_scale, w_out_odd, norm_f):
    n_p, seq, _ = x_prompt.shape
    n_s, steps, _ = x_sample.shape
    depth = norm_in.shape[0]
    past_len = 16384

    xp = x_prompt.reshape(n_p * seq, D_MODEL)
    xs = _sample_rows(x_sample)
    prompt = dict(n_seq=n_p, seq_len=seq, state_conv=None, state_pool=None, first_pos=0)
    sample = dict(n_seq=n_s, seq_len=steps, first_pos=past_len)

    chunk_v_s, conv_p, conv_s, pool_p, pool_s = [], [], [], [], []
    for layer in range(depth):
        i = layer // 2
        if layer % 2 == 0:
            tril = jnp.tril(jnp.ones((CHUNK, CHUNK), bool))
            w_mask = jnp.where(tril[None], sgu_w[i], 0.0)
            mix_p = w_mask.astype(BF16)
            bias_p = jnp.broadcast_to(sgu_b[i][:, :, None], (HEADS, CHUNK, HEAD_DIM))
            eye = jnp.eye(SEQ_TILE, dtype=F32)
            mix_s = jnp.einsum("hts,ab->htasb", w_mask[:, :steps, :steps], eye)
            mix_s = mix_s.reshape(HEADS, TM, TM).astype(BF16)
            bias_s = jnp.broadcast_to(sgu_b[i][:, :steps, None, None],
                                      (HEADS, steps, SEQ_TILE, HEAD_DIM)).reshape(HEADS, TM, HEAD_DIM)
            shared = (v_norm_g[i], v_norm_b[i], conv_w[i], conv_b[i], conv_norm_g[i],
                      conv_norm_b[i], w_out_even[i].astype(BF16))
            w_in = w_in_even[i].astype(BF16)
            sample["state_conv"] = state_conv[i]
            xp, _, cp = _even_layer(xp, prompt, (norm_in[layer], w_in, mix_p, bias_p) + shared)
            xs, vs, cs = _even_layer(xs, sample, (norm_in[layer], w_in, mix_s, bias_s) + shared)
            vs = vs.transpose(1, 0, 2).reshape(n_s * steps, D_MODEL)
            chunk_v_s.append(_sample_unrows(vs, n_s, steps))
            conv_p.append(cp)
            conv_s.append(cs)
        else:
            prm = (norm_in[layer], w_in_odd[i].astype(BF16), pool_w[i].astype(BF16),
                   pool_scale[i], w_out_odd[i].astype(BF16))
            sample["state_pool"] = state_pool[i]
            xp, pp = _odd_layer(xp, prompt, prm)
            xs, ps = _odd_layer(xs, sample, prm)
            pool_p.append(pp)
            pool_s.append(ps)
    y_prompt = _rmsnorm(xp, norm_f, F32).reshape(n_p, seq, D_MODEL)
    y_sample = _sample_unrows(_rmsnorm(xs, norm_f, F32), n_s, steps)
    return (y_prompt, y_sample, jnp.stack(chunk_v_s), jnp.stack(conv_p), jnp.stack(conv_s),
            jnp.stack(pool_p), jnp.stack(pool_s))
```
